```python
import math
import jax, jax.numpy as jnp
from jax import lax
import numpy as np

D_MODEL = 1024
BATCH = 8
SEQ = 2048
DEPTH = 4
DEC_BATCH = 128
DEC_SEQ = 4
PAST_LEN = 16384
PAGE_SIZE = 128

W_A = D_MODEL
N_BLK_A = 8
BLK_A = W_A // N_BLK_A
W_B = D_MODEL
CONV_A = 4
CONV_B = 3
LRU_C = 8.0
ALPHA = (2.0 * DEPTH) ** 0.25
BETA = (8.0 * DEPTH) ** -0.25
LN_EPS = 1e-6
N_IN = 2 * W_A + 4 * W_B + 2 * D_MODEL
SPLITS = (W_A, 2 * W_A, 2 * W_A + W_B, 2 * W_A + 2 * W_B, 2 * W_A + 3 * W_B,
          2 * W_A + 4 * W_B, 2 * W_A + 4 * W_B + D_MODEL)

kernel_name = "hybrid_rglru_shortconv_adaln_deepnorm_step"


def layer_norm(x):
    xf = x.astype(jnp.float32)
    mu = jnp.mean(xf, axis=-1, keepdims=True)
    var = jnp.mean(jnp.square(xf - mu), axis=-1, keepdims=True)
    return (xf - mu) * lax.rsqrt(var + LN_EPS)


def causal_conv(x, buf, w):
    K = w.shape[0]
    T = x.shape[1]
    xp = jnp.concatenate([buf.astype(x.dtype), x], axis=1)
    y = xp[:, 0:T] * w[0]
    for k in range(1, K):
        y = y + xp[:, k:k + T] * w[k]
    return y, xp[:, T:]


def rglru(xa, h0, w_r, b_r, w_i, b_i, lam):
    B, T, _ = xa.shape
    xf = xa.astype(jnp.float32)
    xb = xf.reshape(B, T, N_BLK_A, BLK_A)
    r = jax.nn.sigmoid(jnp.einsum('bthi,hij->bthj', xb, w_r.astype(jnp.float32)).reshape(B, T, W_A) + b_r)
    i = jax.nn.sigmoid(jnp.einsum('bthi,hij->bthj', xb, w_i.astype(jnp.float32)).reshape(B, T, W_A) + b_i)
    log_a = -LRU_C * r * jax.nn.softplus(-lam.astype(jnp.float32))
    a = jnp.exp(log_a)
    mult = jnp.sqrt(-jnp.expm1(2.0 * log_a))
    bterm = mult * (i * xf)
    bterm = bterm.at[:, 0].add(a[:, 0] * h0.astype(jnp.float32))

    def combine(left, right):
        a1, b1 = left
        a2, b2 = right
        return a1 * a2, a2 * b1 + b2

    _, h = lax.associative_scan(combine, (a, bterm), axis=1)
    return h, h[:, -1]


def trunk_layer(x, c, h0, buf_a, buf_b, w_c, b_c, w_in, conv_a_w, conv_a_b, w_r, b_r, w_i, b_i, lam,
                conv_b_w, w_a_out, w_b_out, w_o, ln_g, ln_b):
    dt = x.dtype
    mod = jax.nn.silu(c) @ w_c + b_c
    shift, scale, gate = jnp.split(mod, 3, axis=-1)
    h = (layer_norm(x) * (1.0 + scale[:, None].astype(jnp.float32)) + shift[:, None]).astype(dt)
    u = h @ w_in
    a_x, a_z, b_cg, b_bg, b_x, b_z, g_a, g_b = jnp.split(u, SPLITS, axis=-1)
    xa, new_buf_a = causal_conv(a_x, buf_a, conv_a_w)
    xa = xa + conv_a_b
    hs, h_last = rglru(xa, h0, w_r, b_r, w_i, b_i, lam)
    o_a = (hs.astype(dt) * jax.nn.silu(a_z)) @ w_a_out
    yb, new_buf_b = causal_conv(b_cg * b_x, buf_b, conv_b_w)
    o_b = (b_bg * yb * jax.nn.silu(b_z)) @ w_b_out
    merged = jax.nn.sigmoid(g_a) * o_a + jax.nn.sigmoid(g_b) * o_b
    out = merged @ w_o
    res = ALPHA * x + gate[:, None] * out
    x_new = (layer_norm(res) * ln_g + ln_b).astype(dt)
    return x_new, h_last.astype(dt), new_buf_a, new_buf_b


def setup_inputs(seed: int = 0) -> dict:
    key = jax.random.key(seed)
    ks = jax.random.split(key, 24)
    nrm = jax.random.normal
    f32 = jnp.float32
    a0 = jax.random.uniform(ks[12], (DEPTH, W_A), f32, 0.9, 0.999)
    return {
        "x_prompt": nrm(ks[0], (BATCH, SEQ, D_MODEL), f32),
        "x_sample": nrm(ks[1], (DEC_BATCH, DEC_SEQ, D_MODEL), f32),
        "state_rglru_h": 0.5 * nrm(ks[2], (DEPTH, DEC_BATCH, W_A), f32),
        "state_rglru_conv": nrm(ks[3], (DEPTH, DEC_BATCH, CONV_A - 1, W_A), f32),
        "state_sconv": 0.5 * nrm(ks[4], (DEPTH, DEC_BATCH, CONV_B - 1, W_B), f32),
        "c_prompt": nrm(ks[5], (BATCH, D_MODEL), f32),
        "c_sample": nrm(ks[6], (DEC_BATCH, D_MODEL), f32),
        "w_c": 0.5 * D_MODEL ** -0.5 * nrm(ks[7], (DEPTH, D_MODEL, 3 * D_MODEL), f32),
        "b_c": 0.01 * nrm(ks[8], (DEPTH, 3 * D_MODEL), f32),
        "w_in": D_MODEL ** -0.5 * nrm(ks[9], (DEPTH, D_MODEL, N_IN), f32),
        "conv_a_w": CONV_A ** -0.5 * nrm(ks[10], (DEPTH, CONV_A, W_A), f32),
        "conv_a_b": 0.01 * nrm(ks[11], (DEPTH, W_A), f32),
        "w_r": BLK_A ** -0.5 * nrm(ks[13], (DEPTH, N_BLK_A, BLK_A, BLK_A), f32),
        "b_r": 0.01 * nrm(ks[14], (DEPTH, W_A), f32),
        "w_i": BLK_A ** -0.5 * nrm(ks[15], (DEPTH, N_BLK_A, BLK_A, BLK_A), f32),
        "b_i": 0.01 * nrm(ks[16], (DEPTH, W_A), f32),
        "lru_lambda": jnp.log(a0) - jnp.log1p(-a0),
        "conv_b_w": CONV_B ** -0.5 * nrm(ks[17], (DEPTH, CONV_B, W_B), f32),
        "w_a_out": BETA * W_A ** -0.5 * nrm(ks[18], (DEPTH, W_A, D_MODEL), f32),
        "w_b_out": BETA * W_B ** -0.5 * nrm(ks[19], (DEPTH, W_B, D_MODEL), f32),
        "w_o": BETA * D_MODEL ** -0.5 * nrm(ks[20], (DEPTH, D_MODEL, D_MODEL), f32),
        "ln_g": 1.0 + 0.05 * nrm(ks[21], (DEPTH, D_MODEL), f32),
        "ln_b": 0.01 * nrm(ks[22], (DEPTH, D_MODEL), f32),
    }


def reference(x_prompt, x_sample, state_rglru_h, state_rglru_conv, state_sconv, c_prompt, c_sample,
              w_c, b_c, w_in, conv_a_w, conv_a_b, w_r, b_r, w_i, b_i, lru_lambda, conv_b_w,
              w_a_out, w_b_out, w_o, ln_g, ln_b):
    dt = x_prompt.dtype
    bp = x_prompt.shape[0]
    xp, xs = x_prompt, x_sample
    hp_list, cap_list, cbp_list = [], [], []
    hs_list, cas_list, cbs_list = [], [], []
    for l in range(DEPTH):
        params = (w_c[l], b_c[l], w_in[l], conv_a_w[l], conv_a_b[l], w_r[l], b_r[l], w_i[l], b_i[l],
                  lru_lambda[l], conv_b_w[l], w_a_out[l], w_b_out[l], w_o[l], ln_g[l], ln_b[l])
        h0p = jnp.zeros((bp, W_A), dt)
        bufa_p = jnp.zeros((bp, CONV_A - 1, W_A), dt)
        bufb_p = jnp.zeros((bp, CONV_B - 1, W_B), dt)
        xp, hp, cap, cbp = trunk_layer(xp, c_prompt, h0p, bufa_p, bufb_p, *params)
        xs, hs, cas, cbs = trunk_layer(xs, c_sample, state_rglru_h[l], state_rglru_conv[l], state_sconv[l], *params)
        hp_list.append(hp); cap_list.append(cap); cbp_list.append(cbp)
        hs_list.append(hs); cas_list.append(cas); cbs_list.append(cbs)
    h_prompt = jnp.stack(hp_list)
    conv_a_prompt = jnp.stack(cap_list)
    conv_b_prompt = jnp.stack(cbp_list)
    h_sample = jnp.stack(hs_list)
    conv_a_sample = jnp.stack(cas_list)
    conv_b_sample = jnp.stack(cbs_list)
    return (xp, xs, h_prompt, conv_a_prompt, conv_b_prompt, h_sample, conv_a_sample, conv_b_sample)
```

```python
import functools

import jax
import jax.numpy as jnp
from jax import lax
from jax.experimental import pallas as pl
from jax.experimental.pallas import tpu as pltpu

D_MODEL = 1024
DEPTH = 4
N_BLK_A = 8
BLK_A = D_MODEL // N_BLK_A
CONV_A = 4
CONV_B = 3
LRU_C = 8.0
ALPHA = (2.0 * DEPTH) ** 0.25
LN_EPS = 1e-6

COL_A_X, COL_A_Z, COL_B_CG, COL_B_BG, COL_B_X, COL_B_Z, COL_G_A, COL_G_B = range(8)

ROW_CONV_A_W = 0
ROW_CONV_A_B = 4
ROW_B_R = 5
ROW_B_I = 6
ROW_LAMBDA = 7
ROW_CONV_B_W = 8
ROW_LN_G = 11
ROW_LN_B = 12
VEC_ROWS = 16

ROW_TILE = 512
CHUNK = 16
VMEM_LIMIT_BYTES = 56 * 1024 * 1024

BF16 = jnp.bfloat16
F32 = jnp.float32


def _sigmoid(v):
    return 1.0 / (1.0 + jnp.exp(-v))


def _layer_kernel(nb, x_ref, mod_ref, h0_ref, bufa_ref, bufb_ref,
                  w_in_ref, w_gate_ref, w_ao_ref, w_bo_ref, w_o_ref, vec_ref,
                  y_ref, hl_ref, ta_ref, tb_ref,
                  hb, ca, cb, z, xb, gr, gi, oa, ob, hst, prm):
    rows = x_ref.shape[0]
    n_chunks = rows // CHUNK
    mod_groups = max(nb, CHUNK) // CHUNK
    d = D_MODEL

    def col(k):
        return slice(k * d, (k + 1) * d)

    def vrow(k):
        return vec_ref[pl.ds(k, 1), :]

    def mod_rows(c, part):
        m0 = pl.multiple_of((c % mod_groups) * CHUNK, CHUNK)
        return mod_ref[pl.ds(m0, CHUNK), col(part)]

    def chunk_rows(c, off=0):
        return pl.ds(pl.multiple_of(c * CHUNK, 8) + off, CHUNK)

    def layer_norm(v):
        mu = jnp.mean(v, axis=-1, keepdims=True)
        vc = v - mu
        var = jnp.mean(vc * vc, axis=-1, keepdims=True)
        return vc * lax.rsqrt(var + LN_EPS)

    def project(k, dst, dst_rows=slice(None)):
        dst[dst_rows, :] = jnp.dot(hb[...], w_in_ref[:, col(k)], preferred_element_type=F32)

    @pl.when(pl.program_id(0) == 0)
    def _():
        hst[...] = h0_ref[...]
        ca[rows:, :] = bufa_ref[...]
        cb[rows:, :] = bufb_ref[...]

    ca[0:(CONV_A - 1) * nb, :] = ca[rows:, :]
    cb[0:(CONV_B - 1) * nb, :] = cb[rows:, :]
    prm[0:1, :] = -LRU_C * jax.nn.softplus(-vrow(ROW_LAMBDA))

    def ln_body(c, carry):
        r = chunk_rows(c)
        h = layer_norm(x_ref[r, :]) * (1.0 + mod_rows(c, 1)) + mod_rows(c, 0)
        hb[r, :] = h.astype(BF16)
        return carry

    lax.fori_loop(0, n_chunks, ln_body, 0)

    project(COL_A_X, ca, slice((CONV_A - 1) * nb, None))
    project(COL_A_Z, z)

    def conv_a_body(c, carry):
        acc = ca[chunk_rows(c), :] * vrow(ROW_CONV_A_W)
        for k in range(1, CONV_A):
            acc = acc + ca[chunk_rows(c, k * nb), :] * vrow(ROW_CONV_A_W + k)
        xa = acc + vrow(ROW_CONV_A_B)
        ca[chunk_rows(c), :] = xa
        xb[chunk_rows(c), :] = xa.astype(BF16)
        return carry

    lax.fori_loop(0, n_chunks, conv_a_body, 0)

    for blk in range(N_BLK_A):
        cs = slice(blk * BLK_A, (blk + 1) * BLK_A)
        g = jnp.dot(xb[:, cs], w_gate_ref[blk], preferred_element_type=F32)
        gr[:, cs] = g[:, :BLK_A]
        gi[:, cs] = g[:, BLK_A:]

    def lru_body(c, carry):
        r = chunk_rows(c)
        xa = ca[r, :]
        rg = _sigmoid(gr[r, :] + vrow(ROW_B_R))
        ig = _sigmoid(gi[r, :] + vrow(ROW_B_I))
        log_a = rg * prm[0:1, :]
        a = jnp.exp(log_a)
        th = jnp.tanh(log_a)
        bt = jnp.sqrt(-2.0 * th / (1.0 - th)) * (ig * xa)
        if nb < CHUNK:
            h = hst[...]
            steps = []
            for s in range(CHUNK // nb):
                h = a[s * nb:(s + 1) * nb] * h + bt[s * nb:(s + 1) * nb]
                steps.append(h)
            hst[...] = h
            hs = jnp.concatenate(steps, axis=0)
        else:
            hr = pl.ds(pl.multiple_of((c % mod_groups) * CHUNK, CHUNK), CHUNK)
            hs = a * hst[hr, :] + bt
            hst[hr, :] = hs
        zc = z[r, :]
        xb[r, :] = (hs * (zc * _sigmoid(zc))).astype(BF16)
        return carry

    lax.fori_loop(0, n_chunks, lru_body, 0)
    oa[...] = jnp.dot(xb[...], w_ao_ref[...], preferred_element_type=F32)

    project(COL_B_CG, gr)
    project(COL_B_X, gi)

    def conv_b_body(c, carry):
        r = chunk_rows(c)
        p = gr[r, :] * gi[r, :]
        cb[chunk_rows(c, (CONV_B - 1) * nb), :] = p
        acc = cb[r, :] * vrow(ROW_CONV_B_W)
        acc = acc + cb[chunk_rows(c, nb), :] * vrow(ROW_CONV_B_W + 1)
        acc = acc + p * vrow(ROW_CONV_B_W + 2)
        gr[r, :] = acc
        return carry

    lax.fori_loop(0, n_chunks, conv_b_body, 0)
    project(COL_B_BG, gi)
    project(COL_B_Z, z)

    def gate_b_body(c, carry):
        r = chunk_rows(c)
        zc = z[r, :]
        xb[r, :] = (gi[r, :] * gr[r, :] * (zc * _sigmoid(zc))).astype(BF16)
        return carry

    lax.fori_loop(0, n_chunks, gate_b_body, 0)
    ob[...] = jnp.dot(xb[...], w_bo_ref[...], preferred_element_type=F32)

    project(COL_G_A, gr)
    project(COL_G_B, gi)

    def merge_body(c, carry):
        r = chunk_rows(c)
        m = _sigmoid(gr[r, :]) * oa[r, :] + _sigmoid(gi[r, :]) * ob[r, :]
        xb[r, :] = m.astype(BF16)
        return carry

    lax.fori_loop(0, n_chunks, merge_body, 0)
    z[...] = jnp.dot(xb[...], w_o_ref[...], preferred_element_type=F32)

    def out_body(c, carry):
        r = chunk_rows(c)
        res = ALPHA * x_ref[r, :] + mod_rows(c, 2) * z[r, :]
        y_ref[r, :] = layer_norm(res) * vrow(ROW_LN_G) + vrow(ROW_LN_B)
        return carry

    lax.fori_loop(0, n_chunks, out_body, 0)

    hl_ref[...] = hst[...]
    ta_ref[...] = ca[rows:, :]
    tb_ref[...] = cb[rows:, :]


def _const_spec(shape):
    zeros = (0,) * len(shape)
    return pl.BlockSpec(shape, lambda i: zeros, pipeline_mode=pl.Buffered(1))


def _layer_call(x2d, mod, h0, bufa, bufb, w_in, w_gate, w_ao, w_bo, w_o, vec, *, nb, rows):
    n_rows, d = x2d.shape
    assert n_rows % rows == 0 and rows % CHUNK == 0 and rows % nb == 0
    assert nb % 8 == 0 and (nb >= CHUNK and nb % CHUNK == 0 or CHUNK % nb == 0)
    assert mod.shape[0] == max(nb, CHUNK)
    na, nbb = (CONV_A - 1) * nb, (CONV_B - 1) * nb
    out_shape = (
        jax.ShapeDtypeStruct((n_rows, d), F32),
        jax.ShapeDtypeStruct((nb, d), F32),
        jax.ShapeDtypeStruct((na, d), F32),
        jax.ShapeDtypeStruct((nbb, d), F32),
    )
    row_spec = pl.BlockSpec((rows, d), lambda i: (i, 0))
    in_specs = [row_spec] + [_const_spec(a.shape) for a in
                             (mod, h0, bufa, bufb, w_in, w_gate, w_ao, w_bo, w_o, vec)]
    out_specs = (row_spec, _const_spec((nb, d)), _const_spec((na, d)), _const_spec((nbb, d)))
    scratch = [
        pltpu.VMEM((rows, d), BF16),
        pltpu.VMEM((rows + na, d), F32),
        pltpu.VMEM((rows + nbb, d), F32),
        pltpu.VMEM((rows, d), F32),
        pltpu.VMEM((rows, d), BF16),
        pltpu.VMEM((rows, d), F32),
        pltpu.VMEM((rows, d), F32),
        pltpu.VMEM((rows, d), F32),
        pltpu.VMEM((rows, d), F32),
        pltpu.VMEM((nb, d), F32),
        pltpu.VMEM((8, d), F32),
    ]
    return pl.pallas_call(
        functools.partial(_layer_kernel, nb),
        grid=(n_rows // rows,),
        in_specs=in_specs,
        out_specs=out_specs,
        out_shape=out_shape,
        scratch_shapes=scratch,
        compiler_params=pltpu.CompilerParams(
            dimension_semantics=("arbitrary",), vmem_limit_bytes=VMEM_LIMIT_BYTES),
        name=f"trunk_layer_nb{nb}",
    )(x2d, mod, h0, bufa, bufb, w_in, w_gate, w_ao, w_bo, w_o, vec)


def _mod_kernel(c_ref, w_ref, b_ref, o_ref):
    c = c_ref[...]
    s = (c * _sigmoid(c)).astype(BF16)
    o_ref[0] = jnp.dot(s, w_ref[0].astype(BF16), preferred_element_type=F32) + b_ref[0]


def _mod_call(c_all, w_c, b_c):
    depth, d, n = w_c.shape
    nbatch = c_all.shape[0]
    return pl.pallas_call(
        _mod_kernel,
        grid=(depth, n // d),
        in_specs=[
            pl.BlockSpec((nbatch, d), lambda l, j: (0, 0)),
            pl.BlockSpec((1, d, d), lambda l, j: (l, 0, j)),
            pl.BlockSpec((1, 1, d), lambda l, j: (l, 0, j)),
        ],
        out_specs=pl.BlockSpec((1, nbatch, d), lambda l, j: (l, 0, j)),
        out_shape=jax.ShapeDtypeStruct((depth, nbatch, n), F32),
        compiler_params=pltpu.CompilerParams(dimension_semantics=("arbitrary", "arbitrary")),
        name="adaln_modulation",
    )(c_all, w_c, b_c.reshape(depth, 1, n))


def _time_major(x):
    b, t, d = x.shape
    return jnp.transpose(x, (1, 0, 2)).reshape(t * b, d)


def _batch_major(x2d, nb):
    t = x2d.shape[0] // nb
    return jnp.transpose(x2d.reshape(t, nb, x2d.shape[1]), (1, 0, 2))


def _pack_vec(conv_a_w, conv_a_b, b_r, b_i, lam, conv_b_w, ln_g, ln_b):
    rows = [conv_a_w, conv_a_b[:, None], b_r[:, None], b_i[:, None], lam[:, None],
            conv_b_w, ln_g[:, None], ln_b[:, None]]
    vec = jnp.concatenate(rows, axis=1)
    pad = VEC_ROWS - vec.shape[1]
    return jnp.pad(vec, ((0, 0), (0, pad), (0, 0)))


def kernel(x_prompt, x_sample, state_rglru_h, state_rglru_conv, state_sconv, c_prompt, c_sample, w_c, b_c, w_in, conv_a_w, conv_a_b, w_r, b_r, w_i, b_i, lru_lambda, conv_b_w, w_a_out, w_b_out, w_o, ln_g, ln_b):
    bp = x_prompt.shape[0]
    bs = x_sample.shape[0]
    d = x_prompt.shape[2]
    depth = w_in.shape[0]

    mod_all = _mod_call(jnp.concatenate([c_prompt, c_sample], axis=0), w_c, b_c)
    w_in_b = w_in.astype(BF16)
    w_gate_b = jnp.concatenate([w_r, w_i], axis=-1).astype(BF16)
    w_ao_b, w_bo_b, w_o_b = w_a_out.astype(BF16), w_b_out.astype(BF16), w_o.astype(BF16)
    vec = _pack_vec(conv_a_w, conv_a_b, b_r, b_i, lru_lambda, conv_b_w, ln_g, ln_b)

    xp = _time_major(x_prompt)
    xs = _time_major(x_sample)
    h0p = jnp.zeros((bp, d), F32)
    bufa_p = jnp.zeros(((CONV_A - 1) * bp, d), F32)
    bufb_p = jnp.zeros(((CONV_B - 1) * bp, d), F32)

    outs_p, outs_s = [], []
    for l in range(depth):
        weights = (w_in_b[l], w_gate_b[l], w_ao_b[l], w_bo_b[l], w_o_b[l], vec[l])
        mod_p = mod_all[l, :bp]
        mod_p = jnp.concatenate([mod_p] * (CHUNK // bp), axis=0) if bp < CHUNK else mod_p
        xp, hp, cap, cbp = _layer_call(xp, mod_p, h0p, bufa_p, bufb_p, *weights,
                                       nb=bp, rows=ROW_TILE)
        xs, hs, cas, cbs = _layer_call(xs, mod_all[l, bp:], state_rglru_h[l],
                                       _time_major(state_rglru_conv[l]),
                                       _time_major(state_sconv[l]), *weights,
                                       nb=bs, rows=ROW_TILE)
        outs_p.append((hp, _batch_major(cap, bp), _batch_major(cbp, bp)))
        outs_s.append((hs, _batch_major(cas, bs), _batch_major(cbs, bs)))

    h_p, ca_p, cb_p = (jnp.stack(v) for v in zip(*outs_p))
    h_s, ca_s, cb_s = (jnp.stack(v) for v in zip(*outs_s))
    return (_batch_major(xp, bp), _batch_major(xs, bs), h_p, ca_p, cb_p, h_s, ca_s, cb_s)
```

```python
import functools

import jax
import jax.numpy as jnp
from jax import lax
from jax.experimental import pallas as pl
from jax.experimental.pallas import tpu as pltpu

D_MODEL = 1024
DEPTH = 4
N_BLK_A = 8
BLK_A = D_MODEL // N_BLK_A
CONV_A = 4
CONV_B = 3
LRU_C = 8.0
ALPHA = (2.0 * DEPTH) ** 0.25
LN_EPS = 1e-6

COL_A_X, COL_A_Z, COL_B_CG, COL_B_BG, COL_B_X, COL_B_Z, COL_G_A, COL_G_B = range(8)

ROW_CONV_A_W = 0
ROW_CONV_A_B = 4
ROW_B_R = 5
ROW_B_I = 6
ROW_LAMBDA = 7
ROW_CONV_B_W = 8
ROW_LN_G = 11
ROW_LN_B = 12
VEC_ROWS = 16

ROW_TILE = 512
CHUNK = 16
VMEM_LIMIT_BYTES = 56 * 1024 * 1024

BF16 = jnp.bfloat16
F32 = jnp.float32


def _aligned(v, m):
    return v if isinstance(v, int) else _aligned(v, m)


def _for_chunks(n, body):
    for c in range(n):
        body(c, 0)


def _sigmoid(v):
    return 1.0 / (1.0 + jnp.exp(-v))


def _layer_kernel(nb, x_ref, mod_ref, h0_ref, bufa_ref, bufb_ref,
                  w_in_ref, w_gate_ref, w_ao_ref, w_bo_ref, w_o_ref, vec_ref,
                  y_ref, hl_ref, ta_ref, tb_ref,
                  hb, ca, cb, z, xb, gr, gi, oa, ob, hst, prm):
    rows = x_ref.shape[0]
    n_chunks = rows // CHUNK
    mod_groups = max(nb, CHUNK) // CHUNK
    d = D_MODEL

    def col(k):
        return slice(k * d, (k + 1) * d)

    def vrow(k):
        return vec_ref[pl.ds(k, 1), :]

    def mod_rows(c, part):
        m0 = _aligned((c % mod_groups) * CHUNK, CHUNK)
        return mod_ref[pl.ds(m0, CHUNK), col(part)]

    def chunk_rows(c, off=0):
        return pl.ds(_aligned(c * CHUNK, 8) + off, CHUNK)

    def layer_norm(v):
        mu = jnp.mean(v, axis=-1, keepdims=True)
        vc = v - mu
        var = jnp.mean(vc * vc, axis=-1, keepdims=True)
        return vc * lax.rsqrt(var + LN_EPS)

    def project(k, dst, dst_rows=slice(None)):
        dst[dst_rows, :] = jnp.dot(hb[...], w_in_ref[:, col(k)], preferred_element_type=F32)

    @pl.when(pl.program_id(0) == 0)
    def _():
        hst[...] = h0_ref[...]
        ca[rows:, :] = bufa_ref[...]
        cb[rows:, :] = bufb_ref[...]

    ca[0:(CONV_A - 1) * nb, :] = ca[rows:, :]
    cb[0:(CONV_B - 1) * nb, :] = cb[rows:, :]
    prm[0:1, :] = -LRU_C * jax.nn.softplus(-vrow(ROW_LAMBDA))

    def ln_body(c, carry):
        r = chunk_rows(c)
        h = layer_norm(x_ref[r, :]) * (1.0 + mod_rows(c, 1)) + mod_rows(c, 0)
        hb[r, :] = h.astype(BF16)
        return carry

    _for_chunks(n_chunks,ln_body)

    project(COL_A_X, ca, slice((CONV_A - 1) * nb, None))
    project(COL_A_Z, z)

    def conv_a_body(c, carry):
        acc = ca[chunk_rows(c), :] * vrow(ROW_CONV_A_W)
        for k in range(1, CONV_A):
            acc = acc + ca[chunk_rows(c, k * nb), :] * vrow(ROW_CONV_A_W + k)
        xa = acc + vrow(ROW_CONV_A_B)
        ca[chunk_rows(c), :] = xa
        xb[chunk_rows(c), :] = xa.astype(BF16)
        return carry

    _for_chunks(n_chunks,conv_a_body)

    for blk in range(N_BLK_A):
        cs = slice(blk * BLK_A, (blk + 1) * BLK_A)
        g = jnp.dot(xb[:, cs], w_gate_ref[blk], preferred_element_type=F32)
        gr[:, cs] = g[:, :BLK_A]
        gi[:, cs] = g[:, BLK_A:]

    def lru_body(c, carry):
        r = chunk_rows(c)
        xa = ca[r, :]
        rg = _sigmoid(gr[r, :] + vrow(ROW_B_R))
        ig = _sigmoid(gi[r, :] + vrow(ROW_B_I))
        log_a = rg * prm[0:1, :]
        a = jnp.exp(log_a)
        th = jnp.tanh(log_a)
        bt = jnp.sqrt(-2.0 * th / (1.0 - th)) * (ig * xa)
        if nb < CHUNK:
            h = hst[...]
            steps = []
            for s in range(CHUNK // nb):
                h = a[s * nb:(s + 1) * nb] * h + bt[s * nb:(s + 1) * nb]
                steps.append(h)
            hst[...] = h
            hs = jnp.concatenate(steps, axis=0)
        else:
            hr = pl.ds(_aligned((c % mod_groups) * CHUNK, CHUNK), CHUNK)
            hs = a * hst[hr, :] + bt
            hst[hr, :] = hs
        zc = z[r, :]
        xb[r, :] = (hs * (zc * _sigmoid(zc))).astype(BF16)
        return carry

    _for_chunks(n_chunks,lru_body)
    oa[...] = jnp.dot(xb[...], w_ao_ref[...], preferred_element_type=F32)

    project(COL_B_CG, gr)
    project(COL_B_X, gi)

    def conv_b_body(c, carry):
        r = chunk_rows(c)
        p = gr[r, :] * gi[r, :]
        cb[chunk_rows(c, (CONV_B - 1) * nb), :] = p
        acc = cb[r, :] * vrow(ROW_CONV_B_W)
        acc = acc + cb[chunk_rows(c, nb), :] * vrow(ROW_CONV_B_W + 1)
        acc = acc + p * vrow(ROW_CONV_B_W + 2)
        gr[r, :] = acc
        return carry

    _for_chunks(n_chunks,conv_b_body)
    project(COL_B_BG, gi)
    project(COL_B_Z, z)

    def gate_b_body(c, carry):
        r = chunk_rows(c)
        zc = z[r, :]
        xb[r, :] = (gi[r, :] * gr[r, :] * (zc * _sigmoid(zc))).astype(BF16)
        return carry

    _for_chunks(n_chunks,gate_b_body)
    ob[...] = jnp.dot(xb[...], w_bo_ref[...], preferred_element_type=F32)

    project(COL_G_A, gr)
    project(COL_G_B, gi)

    def merge_body(c, carry):
        r = chunk_rows(c)
        m = _sigmoid(gr[r, :]) * oa[r, :] + _sigmoid(gi[r, :]) * ob[r, :]
        xb[r, :] = m.astype(BF16)
        return carry

    _for_chunks(n_chunks,merge_body)
    z[...] = jnp.dot(xb[...], w_o_ref[...], preferred_element_type=F32)

    def out_body(c, carry):
        r = chunk_rows(c)
        res = ALPHA * x_ref[r, :] + mod_rows(c, 2) * z[r, :]
        y_ref[r, :] = layer_norm(res) * vrow(ROW_LN_G) + vrow(ROW_LN_B)
        return carry

    _for_chunks(n_chunks,out_body)

    hl_ref[...] = hst[...]
    ta_ref[...] = ca[rows:, :]
    tb_ref[...] = cb[rows:, :]


def _const_spec(shape):
    zeros = (0,) * len(shape)
    return pl.BlockSpec(shape, lambda i: zeros, pipeline_mode=pl.Buffered(1))


def _layer_call(x2d, mod, h0, bufa, bufb, w_in, w_gate, w_ao, w_bo, w_o, vec, *, nb, rows):
    n_rows, d = x2d.shape
    assert n_rows % rows == 0 and rows % CHUNK == 0 and rows % nb == 0
    assert nb % 8 == 0 and (nb >= CHUNK and nb % CHUNK == 0 or CHUNK % nb == 0)
    assert mod.shape[0] == max(nb, CHUNK)
    na, nbb = (CONV_A - 1) * nb, (CONV_B - 1) * nb
    out_shape = (
        jax.ShapeDtypeStruct((n_rows, d), F32),
        jax.ShapeDtypeStruct((nb, d), F32),
        jax.ShapeDtypeStruct((na, d), F32),
        jax.ShapeDtypeStruct((nbb, d), F32),
    )
    row_spec = pl.BlockSpec((rows, d), lambda i: (i, 0))
    in_specs = [row_spec] + [_const_spec(a.shape) for a in
                             (mod, h0, bufa, bufb, w_in, w_gate, w_ao, w_bo, w_o, vec)]
    out_specs = (row_spec, _const_spec((nb, d)), _const_spec((na, d)), _const_spec((nbb, d)))
    scratch = [
        pltpu.VMEM((rows, d), BF16),
        pltpu.VMEM((rows + na, d), F32),
        pltpu.VMEM((rows + nbb, d), F32),
        pltpu.VMEM((rows, d), F32),
        pltpu.VMEM((rows, d), BF16),
        pltpu.VMEM((rows, d), F32),
        pltpu.VMEM((rows, d), F32),
        pltpu.VMEM((rows, d), F32),
        pltpu.VMEM((rows, d), F32),
        pltpu.VMEM((nb, d), F32),
        pltpu.VMEM((8, d), F32),
    ]
    return pl.pallas_call(
        functools.partial(_layer_kernel, nb),
        grid=(n_rows // rows,),
        in_specs=in_specs,
        out_specs=out_specs,
        out_shape=out_shape,
        scratch_shapes=scratch,
        compiler_params=pltpu.CompilerParams(
            dimension_semantics=("arbitrary",), vmem_limit_bytes=VMEM_LIMIT_BYTES),
        name=f"trunk_layer_nb{nb}",
    )(x2d, mod, h0, bufa, bufb, w_in, w_gate, w_ao, w_bo, w_o, vec)


def _mod_kernel(c_ref, w_ref, b_ref, o_ref):
    c = c_ref[...]
    s = (c * _sigmoid(c)).astype(BF16)
    o_ref[0] = jnp.dot(s, w_ref[0].astype(BF16), preferred_element_type=F32) + b_ref[0]


def _mod_call(c_all, w_c, b_c):
    depth, d, n = w_c.shape
    nbatch = c_all.shape[0]
    return pl.pallas_call(
        _mod_kernel,
        grid=(depth, n // d),
        in_specs=[
            pl.BlockSpec((nbatch, d), lambda l, j: (0, 0)),
            pl.BlockSpec((1, d, d), lambda l, j: (l, 0, j)),
            pl.BlockSpec((1, 1, d), lambda l, j: (l, 0, j)),
        ],
        out_specs=pl.BlockSpec((1, nbatch, d), lambda l, j: (l, 0, j)),
        out_shape=jax.ShapeDtypeStruct((depth, nbatch, n), F32),
        compiler_params=pltpu.CompilerParams(dimension_semantics=("arbitrary", "arbitrary")),
        name="adaln_modulation",
    )(c_all, w_c, b_c.reshape(depth, 1, n))


def _time_major(x):
    b, t, d = x.shape
    return jnp.transpose(x, (1, 0, 2)).reshape(t * b, d)


def _batch_major(x2d, nb):
    t = x2d.shape[0] // nb
    return jnp.transpose(x2d.reshape(t, nb, x2d.shape[1]), (1, 0, 2))


def _pack_vec(conv_a_w, conv_a_b, b_r, b_i, lam, conv_b_w, ln_g, ln_b):
    rows = [conv_a_w, conv_a_b[:, None], b_r[:, None], b_i[:, None], lam[:, None],
            conv_b_w, ln_g[:, None], ln_b[:, None]]
    vec = jnp.concatenate(rows, axis=1)
    pad = VEC_ROWS - vec.shape[1]
    return jnp.pad(vec, ((0, 0), (0, pad), (0, 0)))


def kernel(x_prompt, x_sample, state_rglru_h, state_rglru_conv, state_sconv, c_prompt, c_sample, w_c, b_c, w_in, conv_a_w, conv_a_b, w_r, b_r, w_i, b_i, lru_lambda, conv_b_w, w_a_out, w_b_out, w_o, ln_g, ln_b):
    bp = x_prompt.shape[0]
    bs = x_sample.shape[0]
    d = x_prompt.shape[2]
    depth = w_in.shape[0]

    mod_all = _mod_call(jnp.concatenate([c_prompt, c_sample], axis=0), w_c, b_c)
    w_in_b = w_in.astype(BF16)
    w_gate_b = jnp.concatenate([w_r, w_i], axis=-1).astype(BF16)
    w_ao_b, w_bo_b, w_o_b = w_a_out.astype(BF16), w_b_out.astype(BF16), w_o.astype(BF16)
    vec = _pack_vec(conv_a_w, conv_a_b, b_r, b_i, lru_lambda, conv_b_w, ln_g, ln_b)

    xp = _time_major(x_prompt)
    xs = _time_major(x_sample)
    h0p = jnp.zeros((bp, d), F32)
    bufa_p = jnp.zeros(((CONV_A - 1) * bp, d), F32)
    bufb_p = jnp.zeros(((CONV_B - 1) * bp, d), F32)

    outs_p, outs_s = [], []
    for l in range(depth):
        weights = (w_in_b[l], w_gate_b[l], w_ao_b[l], w_bo_b[l], w_o_b[l], vec[l])
        mod_p = mod_all[l, :bp]
        mod_p = jnp.concatenate([mod_p] * (CHUNK // bp), axis=0) if bp < CHUNK else mod_p
        xp, hp, cap, cbp = _layer_call(xp, mod_p, h0p, bufa_p, bufb_p, *weights,
                                       nb=bp, rows=ROW_TILE)
        xs, hs, cas, cbs = _layer_call(xs, mod_all[l, bp:], state_rglru_h[l],
                                       _time_major(state_rglru_conv[l]),
                                       _time_major(state_sconv[l]), *weights,
                                       nb=bs, rows=ROW_TILE)
        outs_p.append((hp, _batch_major(cap, bp), _batch_major(cbp, bp)))
        outs_s.append((hs, _batch_major(cas, bs), _batch_major(cbs, bs)))

    h_p, ca_p, cb_p = (jnp.stack(v) for v in zip(*outs_p))
    h_s, ca_s, cb_s = (jnp.stack(v) for v in zip(*outs_s))
    return (_batch_major(xp, bp), _batch_major(xs, bs), h_p, ca_p, cb_p, h_s, ca_s, cb_s)
```

```python
import functools

import jax
import jax.numpy as jnp
from jax import lax
from jax.experimental import pallas as pl
from jax.experimental.pallas import tpu as pltpu

D_MODEL = 1024
DEPTH = 4
N_BLK_A = 8
BLK_A = D_MODEL // N_BLK_A
CONV_A = 4
CONV_B = 3
LRU_C = 8.0
ALPHA = (2.0 * DEPTH) ** 0.25
LN_EPS = 1e-6

COL_A_X, COL_A_Z, COL_B_CG, COL_B_BG, COL_B_X, COL_B_Z, COL_G_A, COL_G_B = range(8)

ROW_CONV_A_W = 0
ROW_CONV_A_B = 4
ROW_B_R = 5
ROW_B_I = 6
ROW_LAMBDA = 7
ROW_CONV_B_W = 8
ROW_LN_G = 11
ROW_LN_B = 12
VEC_ROWS = 16

ROW_TILE = 512
CHUNK = 16
DOT_PIECE = 512
VMEM_LIMIT_BYTES = 56 * 1024 * 1024

BF16 = jnp.bfloat16
F32 = jnp.float32


def _sigmoid(v):
    return 1.0 / (1.0 + jnp.exp(-v))


def _interleave(vpu_items, mxu_items):
    n, m, k = len(vpu_items), len(mxu_items), 0
    for i, item in enumerate(vpu_items):
        while k < m and (k * n) // m <= i:
            mxu_items[k]()
            k += 1
        item()
    for item in mxu_items[k:]:
        item()


def _layer_kernel(nb, x_ref, mod_ref, h0_ref, bufa_ref, bufb_ref,
                  w_in_ref, w_gate_ref, w_ao_ref, w_bo_ref, w_o_ref, vec_ref,
                  y_ref, hl_ref, ta_ref, tb_ref,
                  hb, xb, xb2, ca, cb, b1, b2, b3, b4, b5, b6, b7, hst, prm):
    rows = x_ref.shape[0]
    n_chunks = rows // CHUNK
    mod_groups = max(nb, CHUNK) // CHUNK
    d = D_MODEL
    na, nbb = (CONV_A - 1) * nb, (CONV_B - 1) * nb

    def col(k):
        return slice(k * d, (k + 1) * d)

    def bcast(k):
        return jnp.broadcast_to(vec_ref[k:k + 1, :], (CHUNK, d))

    mod_cache = {}

    def mod_rows(c, part, plus_one=False):
        m0 = (c % mod_groups) * CHUNK
        key = (m0, part) if mod_groups == 1 else None
        if key in mod_cache:
            return mod_cache[key]
        v = mod_ref[m0:m0 + CHUNK, col(part)]
        v = 1.0 + v if plus_one else v
        if key is not None:
            mod_cache[key] = v
        return v

    def rws(c, off=0):
        return slice(c * CHUNK + off, (c + 1) * CHUNK + off)

    def layer_norm(v):
        mu = jnp.mean(v, axis=-1, keepdims=True)
        vc = v - mu
        var = jnp.mean(vc * vc, axis=-1, keepdims=True)
        return vc * lax.rsqrt(var + LN_EPS)

    def dot_items(src, weight, dst, dst_row0=0):
        def piece(n0):
            cs = slice(n0, n0 + DOT_PIECE)
            dst[dst_row0:dst_row0 + rows, cs] = jnp.dot(
                src[...], weight(cs), preferred_element_type=F32)
        return [functools.partial(piece, n0) for n0 in range(0, d, DOT_PIECE)]

    def project(k, dst, dst_row0=0):
        return dot_items(hb, lambda cs: w_in_ref[:, k * d + cs.start:k * d + cs.stop],
                         dst, dst_row0)

    def chunks(body):
        return [functools.partial(body, c) for c in range(n_chunks)]

    @pl.when(pl.program_id(0) == 0)
    def _():
        hst[...] = h0_ref[...]
        ca[rows:, :] = bufa_ref[...]
        cb[rows:, :] = bufb_ref[...]

    ca[0:na, :] = ca[rows:, :]
    cb[0:nbb, :] = cb[rows:, :]
    prm[0:1, :] = -LRU_C * jax.nn.softplus(-vec_ref[ROW_LAMBDA:ROW_LAMBDA + 1, :])

    def ln_body(c):
        h = layer_norm(x_ref[rws(c), :]) * mod_rows(c, 1, plus_one=True) + mod_rows(c, 0)
        hb[rws(c), :] = h.astype(BF16)

    conv_a_w = [bcast(ROW_CONV_A_W + k) for k in range(CONV_A)]
    conv_a_b = bcast(ROW_CONV_A_B)

    def conv_a_body(c):
        acc = ca[rws(c), :] * conv_a_w[0]
        for k in range(1, CONV_A):
            acc = acc + ca[rws(c, k * nb), :] * conv_a_w[k]
        xa = acc + conv_a_b
        ca[rws(c), :] = xa
        xb[rws(c), :] = xa.astype(BF16)

    def gate_items():
        def block(blk):
            cs = slice(blk * BLK_A, (blk + 1) * BLK_A)
            g = jnp.dot(xb[:, cs], w_gate_ref[blk], preferred_element_type=F32)
            b4[:, cs] = g[:, :BLK_A]
            b5[:, cs] = g[:, BLK_A:]
        return [functools.partial(block, blk) for blk in range(N_BLK_A)]

    b_r, b_i = bcast(ROW_B_R), bcast(ROW_B_I)
    neg_c_softplus = jnp.broadcast_to(prm[0:1, :], (CHUNK, d))

    def lru_body(c):
        r = rws(c)
        xa = ca[r, :]
        rg = _sigmoid(b4[r, :] + b_r)
        ig = _sigmoid(b5[r, :] + b_i)
        log_a = rg * neg_c_softplus
        a = jnp.exp(log_a)
        th = jnp.tanh(log_a)
        q = (-2.0 * th) / (1.0 - th)
        mult = jnp.where(q > 0.0, q * lax.rsqrt(q), 0.0)
        bt = mult * (ig * xa)
        if nb < CHUNK:
            h = hst[...]
            steps = []
            for s in range(CHUNK // nb):
                h = a[s * nb:(s + 1) * nb] * h + bt[s * nb:(s + 1) * nb]
                steps.append(h)
            hst[...] = h
            hs = jnp.concatenate(steps, axis=0)
        else:
            hr = slice((c % mod_groups) * CHUNK, (c % mod_groups + 1) * CHUNK)
            hs = a * hst[hr, :] + bt
            hst[hr, :] = hs
        zc = b1[r, :]
        xb[r, :] = (hs * (zc * _sigmoid(zc))).astype(BF16)

    conv_b_w = [bcast(ROW_CONV_B_W + k) for k in range(CONV_B)]

    def conv_b_body(c):
        r = rws(c)
        p = b2[r, :] * b3[r, :]
        cb[rws(c, nbb), :] = p
        acc = cb[r, :] * conv_b_w[0]
        acc = acc + cb[rws(c, nb), :] * conv_b_w[1]
        acc = acc + p * conv_b_w[2]
        b2[r, :] = acc

    def gate_b_body(c):
        r = rws(c)
        zc = b3[r, :]
        xb2[r, :] = (b6[r, :] * b2[r, :] * (zc * _sigmoid(zc))).astype(BF16)

    def merge_a_body(c):
        r = rws(c)
        b7[r, :] = _sigmoid(b7[r, :]) * b1[r, :]
        b4[r, :] = _sigmoid(b4[r, :])

    def merge_b_body(c):
        r = rws(c)
        xb[r, :] = (b7[r, :] + b4[r, :] * b2[r, :]).astype(BF16)

    ln_g, ln_b = bcast(ROW_LN_G), bcast(ROW_LN_B)

    def out_body(c):
        r = rws(c)
        res = ALPHA * x_ref[r, :] + mod_rows(c, 2) * b1[r, :]
        y_ref[r, :] = layer_norm(res) * ln_g + ln_b

    _interleave(chunks(ln_body), [])
    _interleave([], project(COL_A_X, ca, na))
    _interleave(chunks(conv_a_body),
                project(COL_A_Z, b1) + project(COL_B_CG, b2) + project(COL_B_X, b3))
    _interleave(chunks(conv_b_body), gate_items() + project(COL_B_BG, b6))
    _interleave(chunks(lru_body), project(COL_B_Z, b3) + project(COL_G_A, b7))
    _interleave(chunks(gate_b_body),
                dot_items(xb, lambda cs: w_ao_ref[:, cs], b1) + project(COL_G_B, b4))
    _interleave(chunks(merge_a_body), dot_items(xb2, lambda cs: w_bo_ref[:, cs], b2))
    _interleave(chunks(merge_b_body), [])
    _interleave([], dot_items(xb, lambda cs: w_o_ref[:, cs], b1))
    _interleave(chunks(out_body), [])

    hl_ref[...] = hst[...]
    ta_ref[...] = ca[rows:, :]
    tb_ref[...] = cb[rows:, :]


def _const_spec(shape):
    zeros = (0,) * len(shape)
    return pl.BlockSpec(shape, lambda i: zeros, pipeline_mode=pl.Buffered(1))


def _layer_call(x2d, mod, h0, bufa, bufb, w_in, w_gate, w_ao, w_bo, w_o, vec, *, nb, rows):
    n_rows, d = x2d.shape
    assert n_rows % rows == 0 and rows % CHUNK == 0 and rows % nb == 0
    assert nb % 8 == 0 and (nb >= CHUNK and nb % CHUNK == 0 or CHUNK % nb == 0)
    assert mod.shape[0] == max(nb, CHUNK)
    na, nbb = (CONV_A - 1) * nb, (CONV_B - 1) * nb
    out_shape = (
        jax.ShapeDtypeStruct((n_rows, d), F32),
        jax.ShapeDtypeStruct((nb, d), F32),
        jax.ShapeDtypeStruct((na, d), F32),
        jax.ShapeDtypeStruct((nbb, d), F32),
    )
    row_spec = pl.BlockSpec((rows, d), lambda i: (i, 0))
    in_specs = [row_spec] + [_const_spec(a.shape) for a in
                             (mod, h0, bufa, bufb, w_in, w_gate, w_ao, w_bo, w_o, vec)]
    out_specs = (row_spec, _const_spec((nb, d)), _const_spec((na, d)), _const_spec((nbb, d)))
    scratch = (
        [pltpu.VMEM((rows, d), BF16)] * 3
        + [pltpu.VMEM((rows + na, d), F32),
           pltpu.VMEM((rows + nbb, d), F32)]
        + [pltpu.VMEM((rows, d), F32)] * 7
        + [pltpu.VMEM((nb, d), F32),
           pltpu.VMEM((8, d), F32)]
    )
    return pl.pallas_call(
        functools.partial(_layer_kernel, nb),
        grid=(n_rows // rows,),
        in_specs=in_specs,
        out_specs=out_specs,
        out_shape=out_shape,
        scratch_shapes=scratch,
        compiler_params=pltpu.CompilerParams(
            dimension_semantics=("arbitrary",), vmem_limit_bytes=VMEM_LIMIT_BYTES),
        name=f"trunk_layer_nb{nb}",
    )(x2d, mod, h0, bufa, bufb, w_in, w_gate, w_ao, w_bo, w_o, vec)


def _mod_kernel(c_ref, w_ref, b_ref, o_ref):
    c = c_ref[...]
    s = (c * _sigmoid(c)).astype(BF16)
    o_ref[0] = jnp.dot(s, w_ref[0].astype(BF16), preferred_element_type=F32) + b_ref[0]


def _mod_call(c_all, w_c, b_c):
    depth, d, n = w_c.shape
    nbatch = c_all.shape[0]
    return pl.pallas_call(
        _mod_kernel,
        grid=(depth, n // d),
        in_specs=[
            pl.BlockSpec((nbatch, d), lambda l, j: (0, 0)),
            pl.BlockSpec((1, d, d), lambda l, j: (l, 0, j)),
            pl.BlockSpec((1, 1, d), lambda l, j: (l, 0, j)),
        ],
        out_specs=pl.BlockSpec((1, nbatch, d), lambda l, j: (l, 0, j)),
        out_shape=jax.ShapeDtypeStruct((depth, nbatch, n), F32),
        compiler_params=pltpu.CompilerParams(dimension_semantics=("arbitrary", "arbitrary")),
        name="adaln_modulation",
    )(c_all, w_c, b_c.reshape(depth, 1, n))


def _time_major(x):
    b, t, d = x.shape
    return jnp.transpose(x, (1, 0, 2)).reshape(t * b, d)


def _batch_major(x2d, nb):
    t = x2d.shape[0] // nb
    return jnp.transpose(x2d.reshape(t, nb, x2d.shape[1]), (1, 0, 2))


def _pack_vec(conv_a_w, conv_a_b, b_r, b_i, lam, conv_b_w, ln_g, ln_b):
    rows = [conv_a_w, conv_a_b[:, None], b_r[:, None], b_i[:, None], lam[:, None],
            conv_b_w, ln_g[:, None], ln_b[:, None]]
    vec = jnp.concatenate(rows, axis=1)
    pad = VEC_ROWS - vec.shape[1]
    return jnp.pad(vec, ((0, 0), (0, pad), (0, 0)))


def kernel(x_prompt, x_sample, state_rglru_h, state_rglru_conv, state_sconv, c_prompt, c_sample, w_c, b_c, w_in, conv_a_w, conv_a_b, w_r, b_r, w_i, b_i, lru_lambda, conv_b_w, w_a_out, w_b_out, w_o, ln_g, ln_b):
    bp = x_prompt.shape[0]
    bs = x_sample.shape[0]
    d = x_prompt.shape[2]
    depth = w_in.shape[0]

    mod_all = _mod_call(jnp.concatenate([c_prompt, c_sample], axis=0), w_c, b_c)
    w_in_b = w_in.astype(BF16)
    w_gate_b = jnp.concatenate([w_r, w_i], axis=-1).astype(BF16)
    w_ao_b, w_bo_b, w_o_b = w_a_out.astype(BF16), w_b_out.astype(BF16), w_o.astype(BF16)
    vec = _pack_vec(conv_a_w, conv_a_b, b_r, b_i, lru_lambda, conv_b_w, ln_g, ln_b)

    xp = _time_major(x_prompt)
    xs = _time_major(x_sample)
    h0p = jnp.zeros((bp, d), F32)
    bufa_p = jnp.zeros(((CONV_A - 1) * bp, d), F32)
    bufb_p = jnp.zeros(((CONV_B - 1) * bp, d), F32)

    outs_p, outs_s = [], []
    for l in range(depth):
        weights = (w_in_b[l], w_gate_b[l], w_ao_b[l], w_bo_b[l], w_o_b[l], vec[l])
        mod_p = mod_all[l, :bp]
        mod_p = jnp.concatenate([mod_p] * (CHUNK // bp), axis=0) if bp < CHUNK else mod_p
        xp, hp, cap, cbp = _layer_call(xp, mod_p, h0p, bufa_p, bufb_p, *weights,
                                       nb=bp, rows=ROW_TILE)
        xs, hs, cas, cbs = _layer_call(xs, mod_all[l, bp:], state_rglru_h[l],
                                       _time_major(state_rglru_conv[l]),
                                       _time_major(state_sconv[l]), *weights,
                                       nb=bs, rows=ROW_TILE)
        outs_p.append((hp, _batch_major(cap, bp), _batch_major(cbp, bp)))
        outs_s.append((hs, _batch_major(cas, bs), _batch_major(cbs, bs)))

    h_p, ca_p, cb_p = (jnp.stack(v) for v in zip(*outs_p))
    h_s, ca_s, cb_s = (jnp.stack(v) for v in zip(*outs_s))
    return (_batch_major(xp, bp), _batch_major(xs, bs), h_p, ca_p, cb_p, h_s, ca_s, cb_s)
```

```python
import functools

import jax
import jax.numpy as jnp
from jax import lax
from jax.experimental import pallas as pl
from jax.experimental.pallas import tpu as pltpu

D_MODEL = 1024
DEPTH = 4
N_BLK_A = 8
BLK_A = D_MODEL // N_BLK_A
CONV_A = 4
CONV_B = 3
LRU_C = 8.0
ALPHA = (2.0 * DEPTH) ** 0.25
LN_EPS = 1e-6

COL_A_X, COL_A_Z, COL_B_CG, COL_B_BG, COL_B_X, COL_B_Z, COL_G_A, COL_G_B = range(8)

ROW_CONV_A_W = 0
ROW_CONV_A_B = 4
ROW_B_R = 5
ROW_B_I = 6
ROW_LAMBDA = 7
ROW_CONV_B_W = 8
ROW_LN_G = 11
ROW_LN_B = 12
VEC_ROWS = 16

ROW_TILE = 512
CHUNK = 16
DOT_PIECE = 512
VMEM_LIMIT_BYTES = 56 * 1024 * 1024

BF16 = jnp.bfloat16
F32 = jnp.float32


def _sigmoid(v):
    return 0.5 + 0.5 * jnp.tanh(0.5 * v)


def _silu(v):
    hv = 0.5 * v
    return hv + hv * jnp.tanh(hv)


def _interleave(vpu_items, mxu_items):
    n, m, k = len(vpu_items), len(mxu_items), 0
    for i, item in enumerate(vpu_items):
        while k < m and (k * n) // m <= i:
            mxu_items[k]()
            k += 1
        item()
    for item in mxu_items[k:]:
        item()


def _layer_kernel(nb, x_ref, xn_ref, mod_ref, h0_ref, bufa_ref, bufb_ref,
                  w_in_ref, w_gate_ref, w_ao_ref, w_bo_ref, w_o_ref, vec_ref,
                  y_ref, hl_ref, ta_ref, tb_ref,
                  hb, xb, xb2, ca, cb, b1, b2, b3, b4, b5, b6, hst, prm):
    rows = x_ref.shape[0]
    n_chunks = rows // CHUNK
    mod_groups = max(nb, CHUNK) // CHUNK
    d = D_MODEL
    na, nbb = (CONV_A - 1) * nb, (CONV_B - 1) * nb
    step = pl.program_id(0)
    n_tiles = pl.num_programs(0) - 1

    def col(k):
        return slice(k * d, (k + 1) * d)

    def bcast(k):
        return jnp.broadcast_to(vec_ref[k:k + 1, :], (CHUNK, d))

    def rws(c, off=0):
        return slice(c * CHUNK + off, (c + 1) * CHUNK + off)

    def layer_norm(v):
        mu = jnp.mean(v, axis=-1, keepdims=True)
        vc = v - mu
        var = jnp.mean(vc * vc, axis=-1, keepdims=True)
        return vc * lax.rsqrt(var + LN_EPS)

    def chunks(body):
        return [functools.partial(body, c) for c in range(n_chunks)]

    def mod_reader():
        cache = {}

        def mod_rows(c, part, plus_one=False):
            m0 = (c % mod_groups) * CHUNK
            key = (m0, part) if mod_groups == 1 else None
            if key in cache:
                return cache[key]
            v = mod_ref[m0:m0 + CHUNK, col(part)]
            v = 1.0 + v if plus_one else v
            if key is not None:
                cache[key] = v
            return v
        return mod_rows

    def adaln_chunks(src_ref):
        mod_rows = mod_reader()

        def body(c):
            h = layer_norm(src_ref[rws(c), :]) * mod_rows(c, 1, plus_one=True) + mod_rows(c, 0)
            hb[rws(c), :] = h.astype(BF16)
        return chunks(body)

    def post_norm_chunks():
        ln_g, ln_b = bcast(ROW_LN_G), bcast(ROW_LN_B)

        def body(c):
            y_ref[rws(c), :] = layer_norm(b1[rws(c), :]) * ln_g + ln_b
        return chunks(body)

    @pl.when(step == 0)
    def _():
        hst[...] = h0_ref[...]
        ca[rows:, :] = bufa_ref[...]
        cb[rows:, :] = bufb_ref[...]
        b1[...] = jnp.zeros_like(b1)
        _interleave(adaln_chunks(x_ref), [])

    @pl.when(step < n_tiles)
    def _():
        def dot_items(src, weight, dst, dst_row0=0):
            def piece(n0):
                cs = slice(n0, n0 + DOT_PIECE)
                dst[dst_row0:dst_row0 + rows, cs] = jnp.dot(
                    src[...], weight(cs), preferred_element_type=F32)
            return [functools.partial(piece, n0) for n0 in range(0, d, DOT_PIECE)]

        def project(k, dst, dst_row0=0):
            return dot_items(hb, lambda cs: w_in_ref[:, k * d + cs.start:k * d + cs.stop],
                             dst, dst_row0)

        ca[0:na, :] = ca[rows:, :]
        cb[0:nbb, :] = cb[rows:, :]
        prm[0:1, :] = -LRU_C * jax.nn.softplus(-vec_ref[ROW_LAMBDA:ROW_LAMBDA + 1, :])

        conv_a_w = [bcast(ROW_CONV_A_W + k) for k in range(CONV_A)]
        conv_a_b = bcast(ROW_CONV_A_B)

        def conv_a_body(c):
            acc = ca[rws(c), :] * conv_a_w[0]
            for k in range(1, CONV_A):
                acc = acc + ca[rws(c, k * nb), :] * conv_a_w[k]
            xa = acc + conv_a_b
            ca[rws(c), :] = xa
            xb[rws(c), :] = xa.astype(BF16)

        def gate_items():
            def block(blk):
                cs = slice(blk * BLK_A, (blk + 1) * BLK_A)
                g = jnp.dot(xb[:, cs], w_gate_ref[blk], preferred_element_type=F32)
                b4[:, cs] = g[:, :BLK_A]
                b5[:, cs] = g[:, BLK_A:]
            return [functools.partial(block, blk) for blk in range(N_BLK_A)]

        b_r, b_i = bcast(ROW_B_R), bcast(ROW_B_I)
        neg_c_softplus = jnp.broadcast_to(prm[0:1, :], (CHUNK, d))

        def lru_body(c):
            r = rws(c)
            xa = ca[r, :]
            rg = _sigmoid(b4[r, :] + b_r)
            ig = _sigmoid(b5[r, :] + b_i)
            log_a = rg * neg_c_softplus
            a = jnp.exp(log_a)
            th = jnp.tanh(log_a)
            q = (-2.0 * th) / (1.0 - th)
            mult = jnp.where(q > 0.0, q * lax.rsqrt(q), 0.0)
            bt = mult * (ig * xa)
            if nb < CHUNK:
                h = hst[...]
                steps = []
                for s in range(CHUNK // nb):
                    h = a[s * nb:(s + 1) * nb] * h + bt[s * nb:(s + 1) * nb]
                    steps.append(h)
                hst[...] = h
                hs = jnp.concatenate(steps, axis=0)
            else:
                hr = slice((c % mod_groups) * CHUNK, (c % mod_groups + 1) * CHUNK)
                hs = a * hst[hr, :] + bt
                hst[hr, :] = hs
            zc = b1[r, :]
            xb[r, :] = (hs * _silu(zc)).astype(BF16)

        conv_b_w = [bcast(ROW_CONV_B_W + k) for k in range(CONV_B)]

        def conv_b_body(c):
            r = rws(c)
            p = b2[r, :] * b3[r, :]
            cb[rws(c, nbb), :] = p
            acc = cb[r, :] * conv_b_w[0]
            acc = acc + cb[rws(c, nb), :] * conv_b_w[1]
            acc = acc + p * conv_b_w[2]
            b2[r, :] = acc

        def gate_b_body(c):
            r = rws(c)
            zc = b3[r, :]
            xb2[r, :] = (b6[r, :] * b2[r, :] * _silu(zc)).astype(BF16)

        def merge_a_body(c):
            r = rws(c)
            b5[r, :] = _sigmoid(b5[r, :]) * b1[r, :]
            b4[r, :] = _sigmoid(b4[r, :])

        def merge_b_body(c):
            r = rws(c)
            xb[r, :] = (b5[r, :] + b4[r, :] * b2[r, :]).astype(BF16)

        mod_rows = mod_reader()

        def residual_body(c):
            r = rws(c)
            b1[r, :] = ALPHA * x_ref[r, :] + mod_rows(c, 2) * b1[r, :]

        _interleave(post_norm_chunks(), project(COL_A_X, ca, na))
        _interleave(chunks(conv_a_body),
                    project(COL_A_Z, b1) + project(COL_B_CG, b2) + project(COL_B_X, b3))
        _interleave(chunks(conv_b_body), gate_items())
        _interleave(chunks(lru_body), project(COL_B_Z, b3) + project(COL_B_BG, b6))
        _interleave(chunks(gate_b_body),
                    dot_items(xb, lambda cs: w_ao_ref[:, cs], b1)
                    + project(COL_G_A, b5) + project(COL_G_B, b4))
        _interleave(chunks(merge_a_body), dot_items(xb2, lambda cs: w_bo_ref[:, cs], b2))
        _interleave(chunks(merge_b_body), [])
        _interleave(adaln_chunks(xn_ref), dot_items(xb, lambda cs: w_o_ref[:, cs], b1))
        _interleave(chunks(residual_body), [])

        hl_ref[...] = hst[...]
        ta_ref[...] = ca[rows:, :]
        tb_ref[...] = cb[rows:, :]

    @pl.when(step == n_tiles)
    def _():
        _interleave(post_norm_chunks(), [])


def _layer_call(x2d, mod, h0, bufa, bufb, w_in, w_gate, w_ao, w_bo, w_o, vec, *,
                layer, nb, rows, mod_block):
    n_rows, d = x2d.shape
    assert n_rows % rows == 0 and rows % CHUNK == 0 and rows % nb == 0
    assert nb % 8 == 0 and (nb >= CHUNK and nb % CHUNK == 0 or CHUNK % nb == 0)
    n_tiles = n_rows // rows
    na, nbb = (CONV_A - 1) * nb, (CONV_B - 1) * nb
    out_shape = (
        jax.ShapeDtypeStruct((n_rows, d), F32),
        jax.ShapeDtypeStruct((nb, d), F32),
        jax.ShapeDtypeStruct((na, d), F32),
        jax.ShapeDtypeStruct((nbb, d), F32),
    )

    def layer_spec(a, block_rows=None, row_block=0):
        shape = a.shape[1:]
        if block_rows is not None:
            shape = (block_rows,) + shape[1:]
        tail = (0,) * (len(shape) - 1)
        return pl.BlockSpec((None,) + shape, lambda i: (layer, row_block) + tail,
                            pipeline_mode=pl.Buffered(1))

    def out_const(shape):
        return pl.BlockSpec(shape, lambda i: (0, 0), pipeline_mode=pl.Buffered(1))

    row_mode = dict(pipeline_mode=pl.Buffered(1)) if n_tiles == 1 else {}
    in_specs = [
        pl.BlockSpec((rows, d), lambda i: (jnp.minimum(i, n_tiles - 1), 0), **row_mode),
        pl.BlockSpec((rows, d), lambda i: (jnp.minimum(i + 1, n_tiles - 1), 0), **row_mode),
        layer_spec(mod, max(nb, CHUNK), mod_block),
        layer_spec(h0), layer_spec(bufa), layer_spec(bufb),
        layer_spec(w_in), layer_spec(w_gate), layer_spec(w_ao), layer_spec(w_bo),
        layer_spec(w_o), layer_spec(vec),
    ]
    out_specs = (pl.BlockSpec((rows, d), lambda i: (jnp.maximum(i - 1, 0), 0), **row_mode),
                 out_const((nb, d)), out_const((na, d)), out_const((nbb, d)))
    scratch = (
        [pltpu.VMEM((rows, d), BF16)] * 3
        + [pltpu.VMEM((rows + na, d), F32),
           pltpu.VMEM((rows + nbb, d), F32)]
        + [pltpu.VMEM((rows, d), F32)] * 6
        + [pltpu.VMEM((nb, d), F32),
           pltpu.VMEM((8, d), F32)]
    )
    return pl.pallas_call(
        functools.partial(_layer_kernel, nb),
        grid=(n_tiles + 1,),
        in_specs=in_specs,
        out_specs=out_specs,
        out_shape=out_shape,
        scratch_shapes=scratch,
        compiler_params=pltpu.CompilerParams(
            dimension_semantics=("arbitrary",), vmem_limit_bytes=VMEM_LIMIT_BYTES),
        name=f"trunk_layer_nb{nb}",
    )(x2d, x2d, mod, h0, bufa, bufb, w_in, w_gate, w_ao, w_bo, w_o, vec)


def _mod_kernel(c_ref, w_ref, b_ref, o_ref):
    c = c_ref[...]
    s = _silu(c).astype(BF16)
    o_ref[0] = jnp.dot(s, w_ref[0].astype(BF16), preferred_element_type=F32) + b_ref[0]


def _mod_call(c_all, w_c, b_c):
    depth, d, n = w_c.shape
    nbatch = c_all.shape[0]
    return pl.pallas_call(
        _mod_kernel,
        grid=(depth, n // d),
        in_specs=[
            pl.BlockSpec((nbatch, d), lambda l, j: (0, 0)),
            pl.BlockSpec((1, d, d), lambda l, j: (l, 0, j)),
            pl.BlockSpec((1, 1, d), lambda l, j: (l, 0, j)),
        ],
        out_specs=pl.BlockSpec((1, nbatch, d), lambda l, j: (l, 0, j)),
        out_shape=jax.ShapeDtypeStruct((depth, nbatch, n), F32),
        compiler_params=pltpu.CompilerParams(dimension_semantics=("arbitrary", "arbitrary")),
        name="adaln_modulation",
    )(c_all, w_c, b_c.reshape(depth, 1, n))


def _time_major(x):
    *lead, b, t, d = x.shape
    return jnp.swapaxes(x, -3, -2).reshape(*lead, t * b, d)


def _batch_major(x2d, nb):
    *lead, n, d = x2d.shape
    return jnp.swapaxes(x2d.reshape(*lead, n // nb, nb, d), -3, -2)


def _pack_vec(conv_a_w, conv_a_b, b_r, b_i, lam, conv_b_w, ln_g, ln_b):
    rows = [conv_a_w, conv_a_b[:, None], b_r[:, None], b_i[:, None], lam[:, None],
            conv_b_w, ln_g[:, None], ln_b[:, None]]
    vec = jnp.concatenate(rows, axis=1)
    pad = VEC_ROWS - vec.shape[1]
    return jnp.pad(vec, ((0, 0), (0, pad), (0, 0)))


def kernel(x_prompt, x_sample, state_rglru_h, state_rglru_conv, state_sconv, c_prompt, c_sample, w_c, b_c, w_in, conv_a_w, conv_a_b, w_r, b_r, w_i, b_i, lru_lambda, conv_b_w, w_a_out, w_b_out, w_o, ln_g, ln_b):
    bp = x_prompt.shape[0]
    bs = x_sample.shape[0]
    d = x_prompt.shape[2]
    depth = w_in.shape[0]
    assert bs % max(bp, CHUNK) == 0 and CHUNK % bp == 0

    c_all = jnp.concatenate([c_sample] + [c_prompt] * (CHUNK // bp), axis=0)
    mod_all = _mod_call(c_all, w_c, b_c)
    weights = (w_in.astype(BF16), jnp.concatenate([w_r, w_i], axis=-1).astype(BF16),
               w_a_out.astype(BF16), w_b_out.astype(BF16), w_o.astype(BF16),
               _pack_vec(conv_a_w, conv_a_b, b_r, b_i, lru_lambda, conv_b_w, ln_g, ln_b))

    xp = _time_major(x_prompt)
    xs = _time_major(x_sample)
    zero_p = [jnp.zeros((depth, k * bp, d), F32) for k in (1, CONV_A - 1, CONV_B - 1)]
    state_s = (state_rglru_h, _time_major(state_rglru_conv), _time_major(state_sconv))

    outs_p, outs_s = [], []
    for l in range(depth):
        xp, *st_p = _layer_call(xp, mod_all, *zero_p, *weights, layer=l, nb=bp,
                                rows=ROW_TILE, mod_block=bs // CHUNK)
        xs, *st_s = _layer_call(xs, mod_all, *state_s, *weights, layer=l, nb=bs,
                                rows=ROW_TILE, mod_block=0)
        outs_p.append(st_p)
        outs_s.append(st_s)

    h_p, ca_p, cb_p = (jnp.stack(v) for v in zip(*outs_p))
    h_s, ca_s, cb_s = (jnp.stack(v) for v in zip(*outs_s))
    return (_batch_major(xp, bp), _batch_major(xs, bs),
            h_p, _batch_major(ca_p, bp), _batch_major(cb_p, bp),
            h_s, _batch_major(ca_s, bs), _batch_major(cb_s, bs))
```
